```python
import math
import jax, jax.numpy as jnp
from jax import lax
import numpy as np

D_MODEL = 1024
BATCH = 4
SEQ = 4096
DEPTH = 4
DEC_BATCH = 32
DEC_SEQ = 4
PAST_LEN = 8192
PAGE_SIZE = 128

MIX = D_MODEL
HEAD_DIM = 64
GLA_HEADS = 4
GLA_DK = D_MODEL // 16
GLA_DV = D_MODEL // 8
GLA_KEY = GLA_HEADS * GLA_DK
GLA_VAL = GLA_HEADS * GLA_DV
GATE_RANK = 16
GATE_TAU = 16.0
GLA_CHUNK = 32
SWA_HEADS = (MIX - GLA_VAL) // HEAD_DIM
SWA_W = SWA_HEADS * HEAD_DIM
DILATED = ((128, 1), (512, 4), (2048, 16))
MAX_WINDOW = 2048
BAND_BLOCK = 128
P_IN = 2 * GLA_KEY + 2 * GLA_VAL + GATE_RANK + 3 * SWA_W
D_FF = 2816
N_EXPERTS = 8
TOP_K = 2
D_EXPERT = 3584
N_DENSE = (DEPTH + 1) // 2
N_MOE = DEPTH // 2
EPS = 1e-6

kernel_name = 'hymba_gla_dilated_swa_moe_step'


def split_points():
    sizes = (GLA_KEY, GLA_KEY, GLA_VAL, GLA_VAL, GATE_RANK, SWA_W, SWA_W, SWA_W)
    return [int(s) for s in np.cumsum(sizes)[:-1]]


def rms_norm(x, g):
    xf = x.astype(jnp.float32)
    y = xf * lax.rsqrt(jnp.mean(xf * xf, axis=-1, keepdims=True) + EPS)
    return (y * g.astype(jnp.float32)).astype(x.dtype)


def alibi_slopes(n):
    return 2.0 ** (-8.0 * jnp.arange(1, n + 1, dtype=jnp.float32) / n)


def gla_recurrence(q, k, v, log_a, s0, chunk):
    b, l, h, _ = q.shape
    dv = v.shape[-1]
    n = l // chunk

    def to_chunks(t):
        return t.astype(jnp.float32).reshape(b, n, chunk, h, t.shape[-1]).transpose(1, 0, 3, 2, 4)

    qc, kc, vc, gc = to_chunks(q), to_chunks(k), to_chunks(v), to_chunks(log_a)
    causal = jnp.tril(jnp.ones((chunk, chunk), dtype=bool))

    def step(s, inp):
        qi, ki, vi, gi = inp
        cum = jnp.cumsum(gi, axis=2)
        last = cum[:, :, -1:, :]
        q_dec = qi * jnp.exp(cum)
        k_inv = ki * jnp.exp(-cum)
        att = jnp.where(causal, jnp.einsum('bhcd,bhsd->bhcs', q_dec, k_inv), 0.0)
        o = jnp.einsum('bhcs,bhse->bhce', att, vi) + jnp.einsum('bhcd,bhde->bhce', q_dec, s)
        k_tail = ki * jnp.exp(last - cum)
        s_new = s * jnp.exp(last[:, :, 0, :, None]) + jnp.einsum('bhsd,bhse->bhde', k_tail, vi)
        return s_new, o

    s_fin, o = lax.scan(step, s0.astype(jnp.float32), (qc, kc, vc, gc))
    o = o.transpose(1, 0, 3, 2, 4).reshape(b, l, h, dv)
    return o, s_fin


def dilated_band_attention(q, k, v, window, dilation, slopes):
    b, s, h, e = q.shape
    L = s // dilation
    nb = -(-L // BAND_BLOCK)
    lp = nb * BAND_BLOCK
    span = window // dilation

    def residue_blocks(t):
        t = t.reshape(b, L, dilation, h, e).transpose(0, 2, 1, 3, 4)
        t = jnp.pad(t, ((0, 0), (0, 0), (0, lp - L), (0, 0), (0, 0)))
        return t.reshape(b, dilation, nb, BAND_BLOCK, h, e)

    def with_prev(t):
        prev = jnp.pad(t, ((0, 0), (0, 0), (1, 0), (0, 0), (0, 0), (0, 0)))[:, :, :-1]
        return jnp.concatenate([prev, t], axis=3)

    qb = residue_blocks(q)
    kk = with_prev(residue_blocks(k))
    vv = with_prev(residue_blocks(v))
    qi = jnp.arange(BAND_BLOCK)[:, None] + BAND_BLOCK
    kj = jnp.arange(2 * BAND_BLOCK)[None, :]
    delta = qi - kj
    blk = jnp.arange(nb)[:, None, None]
    valid = ((delta >= 0) & (delta <= span))[None] & ((blk - 1) * BAND_BLOCK + kj[None] >= 0)
    scores = jnp.einsum('bdnqhe,bdnkhe->bdnhqk', qb, kk, preferred_element_type=jnp.float32)
    scores = scores - slopes[:, None, None] * (delta * dilation).astype(jnp.float32)
    scores = jnp.where(valid[None, None, :, None], scores, -jnp.inf)
    lse = jax.nn.logsumexp(scores, axis=-1)
    p = jnp.exp(scores - lse[..., None])
    out = jnp.einsum('bdnhqk,bdnkhe->bdnqhe', p, vv.astype(jnp.float32))
    out = out.reshape(b, dilation, lp, h, e)[:, :, :L].transpose(0, 2, 1, 3, 4).reshape(b, s, h, e)
    lse = lse.transpose(0, 1, 2, 4, 3).reshape(b, dilation, lp, h)[:, :, :L].transpose(0, 2, 1, 3).reshape(b, s, h)
    return out, lse


def dilated_window_decode(q, k_all, v_all, n_past, window, dilation, slopes):
    t = q.shape[1]
    j = jnp.arange(window // dilation + 1)
    idx = n_past + jnp.arange(t)[:, None] - j[None, :] * dilation
    valid = idx >= 0
    idx = jnp.maximum(idx, 0)
    kg = k_all[:, idx]
    vg = v_all[:, idx]
    scores = jnp.einsum('bthe,btjhe->bhtj', q, kg, preferred_element_type=jnp.float32)
    scores = scores - slopes[:, None, None] * (j * dilation).astype(jnp.float32)
    scores = jnp.where(valid, scores, -jnp.inf)
    lse = jax.nn.logsumexp(scores, axis=-1)
    p = jnp.exp(scores - lse[..., None])
    out = jnp.einsum('bhtj,btjhe->bthe', p, vg.astype(jnp.float32))
    return out, lse.transpose(0, 2, 1)


def merge_by_denominator(parts):
    outs = jnp.stack([o for o, _ in parts], 0)
    w = jax.nn.softmax(jnp.stack([l for _, l in parts], 0), axis=0)
    return jnp.sum(w[..., None] * outs, axis=0)


def token_mixer(x, norm_g, w_in, w_gate_lr, b_gate, g_gla_out, g_q, g_k, g_swa_out, w_out,
                gla_s0, buf_k, buf_v):
    b, l, _ = x.shape
    h = rms_norm(x, norm_g)
    proj = h @ w_in
    gq, gk, gv, gr, glr, sq, sk, sv = jnp.split(proj, split_points(), axis=-1)
    gq = gq.reshape(b, l, GLA_HEADS, GLA_DK) * (GLA_DK ** -0.5)
    gk = gk.reshape(b, l, GLA_HEADS, GLA_DK)
    gv = gv.reshape(b, l, GLA_HEADS, GLA_DV)
    log_a = jax.nn.log_sigmoid((glr @ w_gate_lr + b_gate).astype(jnp.float32)) / GATE_TAU
    log_a = log_a.reshape(b, l, GLA_HEADS, GLA_DK)
    o_gla, s_fin = gla_recurrence(gq, gk, gv, log_a, gla_s0, math.gcd(l, GLA_CHUNK))
    o_gla = rms_norm(o_gla, g_gla_out.reshape(GLA_HEADS, GLA_DV)).reshape(b, l, GLA_VAL)
    o_gla = (o_gla * jax.nn.silu(gr.astype(jnp.float32))).astype(x.dtype)
    sq = rms_norm(sq.reshape(b, l, SWA_HEADS, HEAD_DIM), g_q) * (HEAD_DIM ** -0.5)
    sk = rms_norm(sk.reshape(b, l, SWA_HEADS, HEAD_DIM), g_k)
    sv = sv.reshape(b, l, SWA_HEADS, HEAD_DIM)
    slopes = alibi_slopes(SWA_HEADS)
    if buf_k is None:
        parts = [dilated_band_attention(sq, sk, sv, w, d, slopes) for w, d in DILATED]
        keep = min(MAX_WINDOW, l)
        new_k, new_v = sk[:, l - keep:], sv[:, l - keep:]
    else:
        n_past = buf_k.shape[1]
        k_all = jnp.concatenate([buf_k.astype(sk.dtype), sk], axis=1)
        v_all = jnp.concatenate([buf_v.astype(sv.dtype), sv], axis=1)
        parts = [dilated_window_decode(sq, k_all, v_all, n_past, w, d, slopes) for w, d in DILATED]
        keep = min(MAX_WINDOW, n_past + l)
        new_k, new_v = k_all[:, n_past + l - keep:], v_all[:, n_past + l - keep:]
    o_swa = merge_by_denominator(parts)
    o_swa = rms_norm(o_swa, g_swa_out.reshape(SWA_HEADS, HEAD_DIM)).reshape(b, l, SWA_W).astype(x.dtype)
    y = x + jnp.concatenate([o_gla, o_swa], axis=-1) @ w_out
    return y, s_fin, new_k, new_v


def swiglu(h, wg, wu, wd):
    return (jax.nn.silu(h @ wg) * (h @ wu)) @ wd


def moe_ffn(h, w_router, wg, wu, wd):
    b, l, d = h.shape
    t = h.reshape(b * l, d)
    logits = (t @ w_router).astype(jnp.float32)
    top_v, top_i = lax.top_k(logits, TOP_K)
    gates = jax.nn.softmax(top_v, axis=-1)
    dense_gates = jnp.sum(jax.nn.one_hot(top_i, N_EXPERTS, dtype=jnp.float32) * gates[..., None], axis=1)
    y = jnp.zeros((b * l, d), jnp.float32)
    for e in range(N_EXPERTS):
        y = y + dense_gates[:, e:e + 1] * swiglu(t, wg[e], wu[e], wd[e]).astype(jnp.float32)
    return y.reshape(b, l, d).astype(h.dtype)


def setup_inputs(seed: int = 0) -> dict:
    key = jax.random.key(seed)
    ks = jax.random.split(key, 24)

    def nrm(i, shape, scale):
        return jax.random.normal(ks[i], shape, jnp.float32) * scale

    wb = min(MAX_WINDOW, PAST_LEN)
    return {
        'x_prompt': nrm(0, (BATCH, SEQ, D_MODEL), 1.0),
        'x_sample': nrm(1, (DEC_BATCH, DEC_SEQ, D_MODEL), 1.0),
        'cache_swa_k': nrm(2, (DEPTH, DEC_BATCH, wb, SWA_HEADS, HEAD_DIM), 1.0),
        'cache_swa_v': nrm(3, (DEPTH, DEC_BATCH, wb, SWA_HEADS, HEAD_DIM), 1.0),
        'state_gla': nrm(4, (DEPTH, DEC_BATCH, GLA_HEADS, GLA_DK, GLA_DV), 0.5),
        'norm_mix': 1.0 + nrm(5, (DEPTH, D_MODEL), 0.02),
        'w_in': nrm(6, (DEPTH, D_MODEL, P_IN), D_MODEL ** -0.5),
        'w_gate_lr': nrm(7, (DEPTH, GATE_RANK, GLA_KEY), GATE_RANK ** -0.5),
        'b_gate': nrm(8, (DEPTH, GLA_KEY), 0.1),
        'g_gla_out': 1.0 + nrm(9, (DEPTH, GLA_VAL), 0.02),
        'g_q': 1.0 + nrm(10, (DEPTH, HEAD_DIM), 0.02),
        'g_k': 1.0 + nrm(11, (DEPTH, HEAD_DIM), 0.02),
        'g_swa_out': 1.0 + nrm(12, (DEPTH, SWA_W), 0.02),
        'w_out': nrm(13, (DEPTH, MIX, D_MODEL), MIX ** -0.5),
        'norm_ffn': 1.0 + nrm(14, (DEPTH, D_MODEL), 0.02),
        'w_ffn_gate': nrm(15, (N_DENSE, D_MODEL, D_FF), D_MODEL ** -0.5),
        'w_ffn_up': nrm(16, (N_DENSE, D_MODEL, D_FF), D_MODEL ** -0.5),
        'w_ffn_down': nrm(17, (N_DENSE, D_FF, D_MODEL), D_FF ** -0.5),
        'w_router': nrm(18, (N_MOE, D_MODEL, N_EXPERTS), D_MODEL ** -0.5),
        'w_exp_gate': nrm(19, (N_MOE, N_EXPERTS, D_MODEL, D_EXPERT), D_MODEL ** -0.5),
        'w_exp_up': nrm(20, (N_MOE, N_EXPERTS, D_MODEL, D_EXPERT), D_MODEL ** -0.5),
        'w_exp_down': nrm(21, (N_MOE, N_EXPERTS, D_EXPERT, D_MODEL), D_EXPERT ** -0.5),
    }


def reference(x_prompt, x_sample, cache_swa_k, cache_swa_v, state_gla, norm_mix, w_in, w_gate_lr,
              b_gate, g_gla_out, g_q, g_k, g_swa_out, w_out, norm_ffn, w_ffn_gate, w_ffn_up,
              w_ffn_down, w_router, w_exp_gate, w_exp_up, w_exp_down):
    def layer(x, i, s0, buf_k, buf_v):
        x, s_fin, nk, nv = token_mixer(x, norm_mix[i], w_in[i], w_gate_lr[i], b_gate[i], g_gla_out[i],
                                       g_q[i], g_k[i], g_swa_out[i], w_out[i], s0, buf_k, buf_v)
        h = rms_norm(x, norm_ffn[i])
        j = i // 2
        if i % 2 == 0:
            x = x + swiglu(h, w_ffn_gate[j], w_ffn_up[j], w_ffn_down[j])
        else:
            x = x + moe_ffn(h, w_router[j], w_exp_gate[j], w_exp_up[j], w_exp_down[j])
        return x, s_fin, nk, nv

    yp, ys = x_prompt, x_sample
    pk, pv, ps, sk, sv, ss = [], [], [], [], [], []
    zero_state = jnp.zeros((x_prompt.shape[0], GLA_HEADS, GLA_DK, GLA_DV), jnp.float32)
    for i in range(DEPTH):
        yp, s_p, k_p, v_p = layer(yp, i, zero_state, None, None)
        ys, s_s, k_s, v_s = layer(ys, i, state_gla[i], cache_swa_k[i], cache_swa_v[i])
        pk.append(k_p); pv.append(v_p); ps.append(s_p.astype(x_prompt.dtype))
        sk.append(k_s); sv.append(v_s); ss.append(s_s.astype(state_gla.dtype))
    return (yp, ys, jnp.stack(pk), jnp.stack(pv), jnp.stack(ps), jnp.stack(sk), jnp.stack(sv), jnp.stack(ss))
```

```python
import functools

import numpy as np
import jax
import jax.numpy as jnp
from jax import lax
from jax.experimental import pallas as pl
from jax.experimental.pallas import tpu as pltpu

F32 = jnp.float32
BF16 = jnp.bfloat16

D_MODEL = 1024
BATCH = 4
SEQ = 4096
DEPTH = 4
DEC_BATCH = 32
DEC_SEQ = 4
HEAD_DIM = 64
GLA_HEADS = 4
GLA_DK = 64
GLA_DV = 128
GLA_KEY = GLA_HEADS * GLA_DK
GLA_VAL = GLA_HEADS * GLA_DV
GATE_RANK = 16
GATE_TAU = 16.0
GLA_CHUNK = 32
SWA_HEADS = 8
SWA_W = SWA_HEADS * HEAD_DIM
DILATIONS = (1, 4, 16)
BAND = 128
MAX_WINDOW = 2048
D_FF = 2816
N_EXPERTS = 8
D_EXPERT = 3584
EPS = 1e-6
NEG = -1e30

T_PROMPT = BATCH * SEQ
T_SAMPLE = DEC_BATCH * DEC_SEQ
TOK_TILE = 512
T_ALL = 16896
DEC_PAD = 16
MOE_TILE = 256
FF_CHUNK = 256
EXP_CHUNK = 512
VMEM_LIMIT = 56 * 1024 * 1024

_NT = (((1,), (1,)), ((), ()))
_TN = (((0,), (0,)), ((), ()))


def _dot(a, b):
    return jnp.dot(a, b, preferred_element_type=F32)


def _rms(x, g):
    return x * lax.rsqrt(jnp.mean(x * x, axis=-1, keepdims=True) + EPS) * g


def _split_bf16(a):
    hi = a.astype(BF16)
    lo = (a - hi.astype(F32)).astype(BF16)
    return hi, lo


def _head_rms(a, g, bd):
    hi, lo = _split_bf16(a * a)
    ms = _dot(hi, bd) + _dot(lo, bd)
    return a * lax.rsqrt(ms + EPS) * g


def _silu(a):
    return a / (1.0 + jnp.exp(-a))


def _params(*sem):
    return pltpu.CompilerParams(dimension_semantics=sem, vmem_limit_bytes=VMEM_LIMIT)


def _row_spec(width, tile=TOK_TILE):
    return pl.BlockSpec((tile, width), lambda i: (i, 0))


def _full_spec(shape):
    nd = len(shape)
    return pl.BlockSpec(shape, lambda *_: (0,) * nd)


def _inproj_kernel(x_ref, g_ref, wa_ref, wlr_ref, ws_ref, wgate_ref, bgate_ref, bd_ref, gq_ref, gk_ref,
                   q_o, k_o, v_o, r_o, la_o, sq_o, sk_o, sv_o, skb_o, svb_o):
    h = _rms(x_ref[...], g_ref[...]).astype(BF16)
    q_o[...] = _dot(h, wa_ref[:, 0:GLA_KEY]) * (GLA_DK ** -0.5)
    k_o[...] = _dot(h, wa_ref[:, GLA_KEY:2 * GLA_KEY])
    v_o[...] = _dot(h, wa_ref[:, 2 * GLA_KEY:2 * GLA_KEY + GLA_VAL])
    r_o[...] = _dot(h, wa_ref[:, 2 * GLA_KEY + GLA_VAL:])
    z = _dot(_dot(h, wlr_ref[...]).astype(BF16), wgate_ref[...]) + bgate_ref[...]
    la_o[...] = (jnp.minimum(z, 0.0) - jnp.log1p(jnp.exp(-jnp.abs(z)))) * (1.0 / GATE_TAU)
    bd = bd_ref[...]
    sq = _head_rms(_dot(h, ws_ref[:, 0:SWA_W]), gq_ref[...], bd) * (HEAD_DIM ** -0.5)
    sq_o[...] = sq.astype(BF16)
    sk = _head_rms(_dot(h, ws_ref[:, SWA_W:2 * SWA_W]), gk_ref[...], bd)
    sk_o[...] = sk
    skb_o[...] = sk.astype(BF16)
    sv = _dot(h, ws_ref[:, 2 * SWA_W:])
    sv_o[...] = sv
    svb_o[...] = sv.astype(BF16)


def _inproj(x, g, wa, wlr, ws, wgate, bgate, bd, gq, gk):
    widths = (GLA_KEY, GLA_KEY, GLA_VAL, GLA_VAL, GLA_KEY, SWA_W, SWA_W, SWA_W, SWA_W, SWA_W)
    dtypes = (F32, F32, F32, F32, F32, BF16, F32, F32, BF16, BF16)
    consts = (g, wa, wlr, ws, wgate, bgate, bd, gq, gk)
    return pl.pallas_call(
        _inproj_kernel,
        out_shape=[jax.ShapeDtypeStruct((T_ALL, w), d) for w, d in zip(widths, dtypes)],
        grid=(T_ALL // TOK_TILE,),
        in_specs=[_row_spec(D_MODEL)] + [_full_spec(c.shape) for c in consts],
        out_specs=[_row_spec(w) for w in widths],
        compiler_params=_params("parallel"),
        name="inproj",
    )(x, *consts)


def _gla_kernel(q_ref, k_ref, g_ref, v_ref, s0_ref, o_ref, sfin_ref, st_ref, *, chunk, n_chunks):
    j = pl.program_id(1)

    @pl.when(j == 0)
    def _():
        st_ref[...] = s0_ref[0]

    row = lax.broadcasted_iota(jnp.int32, (chunk, chunk), 0)
    col = lax.broadcasted_iota(jnp.int32, (chunk, chunk), 1)
    causal = row >= col
    tri = jnp.where(causal, 1.0, 0.0).astype(BF16)

    def body(i, carry):
        c0 = pl.multiple_of(i * chunk, chunk)
        rows = pl.ds(c0, chunk)
        q = q_ref[rows, :]
        k = k_ref[rows, :]
        v = v_ref[rows, :]
        ghi, glo = _split_bf16(g_ref[rows, :])
        cum = _dot(tri, ghi) + _dot(tri, glo)
        last = cum[chunk - 1:chunk, :]
        q_dec = q * jnp.exp(cum)
        k_inv = k * jnp.exp(-cum)
        k_tail = k * jnp.exp(last - cum)
        decay = jnp.exp(last)
        for h in range(GLA_HEADS):
            ks = slice(h * GLA_DK, (h + 1) * GLA_DK)
            vs = slice(h * GLA_DV, (h + 1) * GLA_DV)
            qh = q_dec[:, ks].astype(BF16)
            vh = v[:, vs].astype(BF16)
            att = lax.dot_general(qh, k_inv[:, ks].astype(BF16), _NT, preferred_element_type=F32)
            att = jnp.where(causal, att, 0.0).astype(BF16)
            st = st_ref[h]
            o_ref[rows, vs] = _dot(att, vh) + lax.dot_general(
                qh, st.astype(BF16), _NT, preferred_element_type=F32)
            st_ref[h] = st * decay[:, ks] + lax.dot_general(
                vh, k_tail[:, ks].astype(BF16), _TN, preferred_element_type=F32)
        return carry

    lax.fori_loop(0, n_chunks, body, 0)

    @pl.when(j == pl.num_programs(1) - 1)
    def _():
        sfin_ref[0] = st_ref[...]


def _gla(q, k, g, v, s0t, *, batch, seq_len, seq_block, chunk, name):
    nb = seq_len // seq_block
    kern = functools.partial(_gla_kernel, chunk=chunk, n_chunks=seq_block // chunk)
    rows = lambda w: pl.BlockSpec((seq_block, w), lambda b, j: (b * nb + j, 0))
    state = pl.BlockSpec((1, GLA_HEADS, GLA_DV, GLA_DK), lambda b, j: (b, 0, 0, 0))
    return pl.pallas_call(
        kern,
        out_shape=[jax.ShapeDtypeStruct((batch * seq_len, GLA_VAL), F32),
                   jax.ShapeDtypeStruct((batch, GLA_HEADS, GLA_DV, GLA_DK), F32)],
        grid=(batch, nb),
        in_specs=[rows(GLA_KEY), rows(GLA_KEY), rows(GLA_KEY), rows(GLA_VAL), state],
        out_specs=[rows(GLA_VAL), state],
        scratch_shapes=[pltpu.VMEM((GLA_HEADS, GLA_DV, GLA_DK), F32)],
        compiler_params=_params("parallel", "arbitrary"),
        name=name,
    )(q, k, g, v, s0t)


def _alibi_slopes():
    return (2.0 ** (-8.0 * np.arange(1, SWA_HEADS + 1, dtype=np.float32) / SWA_HEADS)).astype(np.float32)


def _band_bias(dilation):
    qi = np.arange(BAND)[:, None] + BAND
    kj = np.arange(2 * BAND)[None, :]
    delta = qi - kj
    valid = (delta >= 0) & (delta <= BAND)
    bias = -_alibi_slopes()[:, None, None] * (delta * dilation).astype(np.float32)[None]
    first = np.where((valid & (kj >= BAND))[None], bias, NEG)
    rest = np.where(valid[None], bias, NEG)
    return np.stack([first, rest]).astype(np.float32)


def _band_attn_kernel(q_ref, kp_ref, ko_ref, vp_ref, vo_ref, b_ref, o_ref, l_ref):
    for h in range(SWA_HEADS):
        sl = slice(h * HEAD_DIM, (h + 1) * HEAD_DIM)
        qh = q_ref[0, :, sl]
        sp = lax.dot_general(qh, kp_ref[0, :, sl], _NT, preferred_element_type=F32) + b_ref[0, h, :, 0:BAND]
        so = lax.dot_general(qh, ko_ref[0, :, sl], _NT, preferred_element_type=F32) + b_ref[0, h, :, BAND:]
        m = jnp.maximum(jnp.max(sp, axis=-1, keepdims=True), jnp.max(so, axis=-1, keepdims=True))
        pp = jnp.exp(sp - m)
        po = jnp.exp(so - m)
        den = jnp.sum(pp, axis=-1, keepdims=True) + jnp.sum(po, axis=-1, keepdims=True)
        acc = _dot(pp.astype(BF16), vp_ref[0, :, sl]) + _dot(po.astype(BF16), vo_ref[0, :, sl])
        o_ref[0, :, sl] = acc / den
        l_ref[0, :, sl] = jnp.broadcast_to(m + jnp.log(den), (BAND, HEAD_DIM))


def _band_attn(q, k, v, dilation):
    steps = SEQ // dilation
    nb = steps // BAND
    bias = jnp.asarray(_band_bias(dilation))
    own = pl.BlockSpec((1, BAND, SWA_W), lambda b, r, n: (b, n, r))
    prev = pl.BlockSpec((1, BAND, SWA_W), lambda b, r, n: (b, jnp.maximum(n - 1, 0), r))
    bspec = pl.BlockSpec((1, SWA_HEADS, BAND, 2 * BAND), lambda b, r, n: (jnp.minimum(n, 1), 0, 0, 0))
    shape = jax.ShapeDtypeStruct((BATCH, steps, dilation * SWA_W), F32)
    return pl.pallas_call(
        _band_attn_kernel,
        out_shape=[shape, shape],
        grid=(BATCH, dilation, nb),
        in_specs=[own, prev, own, prev, own, bspec],
        out_specs=[own, own],
        compiler_params=_params("parallel", "parallel", "arbitrary"),
        name=f"band_attn_d{dilation}",
    )(q, k, k, v, v, bias)


def _merge_kernel(o1, o4, o16, l1, l4, l16, g_ref, bd_ref, out_ref):
    a, b, c = l1[...], l4[...], l16[...]
    m = jnp.maximum(jnp.maximum(a, b), c)
    ea, eb, ec = jnp.exp(a - m), jnp.exp(b - m), jnp.exp(c - m)
    o = (ea * o1[...] + eb * o4[...] + ec * o16[...]) / (ea + eb + ec)
    out_ref[...] = _head_rms(o, g_ref[...], bd_ref[...])


def _merge(outs, lses, g, bd):
    return pl.pallas_call(
        _merge_kernel,
        out_shape=jax.ShapeDtypeStruct((T_PROMPT, SWA_W), F32),
        grid=(T_PROMPT // TOK_TILE,),
        in_specs=[_row_spec(SWA_W)] * 6 + [_full_spec(g.shape), _full_spec(bd.shape)],
        out_specs=_row_spec(SWA_W),
        compiler_params=_params("parallel"),
        name="merge_dilations",
    )(*outs, *lses, g, bd)


def _multiplicity(delta):
    m = np.zeros(delta.shape, np.float32)
    for d in DILATIONS:
        m += ((delta >= 0) & (delta % d == 0) & (delta <= BAND * d)).astype(np.float32)
    return m


def _decode_tables():
    t = np.arange(DEC_PAD)[:, None]
    slopes = _alibi_slopes()[:, None, None]
    d_buf = MAX_WINDOW + t - np.arange(MAX_WINDOW)[None, :]
    m_buf = _multiplicity(d_buf)
    b_buf = np.where(m_buf[None] > 0, -slopes * d_buf[None].astype(np.float32), NEG)
    jn = np.arange(DEC_PAD)[None, :]
    d_new = t - jn
    m_new = _multiplicity(d_new) * (jn < DEC_SEQ)
    b_new = np.where(m_new[None] > 0, -slopes * d_new[None].astype(np.float32), NEG)
    return (b_buf.astype(np.float32), m_buf.astype(np.float32),
            b_new.astype(np.float32), m_new.astype(np.float32))


def _dec_attn_kernel(q_ref, kc_ref, vc_ref, kn_ref, vn_ref, bb_ref, mb_ref, bn_ref, mn_ref, g_ref, o_ref):
    q = q_ref[0].astype(BF16)
    kn = kn_ref[0].astype(BF16)
    vn = vn_ref[0].astype(BF16)
    for h in range(SWA_HEADS):
        sl = slice(h * HEAD_DIM, (h + 1) * HEAD_DIM)
        qh = q[:, sl]
        kc = kc_ref[0, :, sl].astype(BF16)
        vc = vc_ref[0, :, sl].astype(BF16)
        sb = lax.dot_general(qh, kc, _NT, preferred_element_type=F32) + bb_ref[h]
        sn = lax.dot_general(qh, kn[:, sl], _NT, preferred_element_type=F32) + bn_ref[h]
        m = jnp.maximum(jnp.max(sb, axis=-1, keepdims=True), jnp.max(sn, axis=-1, keepdims=True))
        pb = jnp.exp(sb - m) * mb_ref[...]
        pn = jnp.exp(sn - m) * mn_ref[...]
        den = jnp.sum(pb, axis=-1, keepdims=True) + jnp.sum(pn, axis=-1, keepdims=True)
        o = (_dot(pb.astype(BF16), vc) + _dot(pn.astype(BF16), vn[:, sl])) / den
        o_ref[0, :, sl] = _rms(o, g_ref[:, sl])


def _dec_attn(q, kc, vc, kn, vn, g):
    tables = [jnp.asarray(t) for t in _decode_tables()]
    new = pl.BlockSpec((1, DEC_PAD, SWA_W), lambda b: (b, 0, 0))
    buf = pl.BlockSpec((1, MAX_WINDOW, SWA_W), lambda b: (b, 0, 0))
    return pl.pallas_call(
        _dec_attn_kernel,
        out_shape=jax.ShapeDtypeStruct((DEC_BATCH, DEC_PAD, SWA_W), F32),
        grid=(DEC_BATCH,),
        in_specs=[new, buf, buf, new, new] + [_full_spec(t.shape) for t in tables] + [_full_spec(g.shape)],
        out_specs=new,
        compiler_params=_params("parallel"),
        name="decode_attn",
    )(q, kc, vc, kn, vn, *tables, g)


def _outproj_kernel(x_ref, og_ref, r_ref, os_ref, gg_ref, wg_ref, ws_ref, y_ref):
    acc = x_ref[...] + _dot(os_ref[...].astype(BF16), ws_ref[...])
    for h in range(GLA_HEADS):
        vs = slice(h * GLA_DV, (h + 1) * GLA_DV)
        o = _rms(og_ref[:, vs], gg_ref[:, vs]) * _silu(r_ref[:, vs])
        acc = acc + _dot(o.astype(BF16), wg_ref[vs, :])
    y_ref[...] = acc


def _outproj(x, og, r, osw, gg, w_gla, w_swa):
    consts = (gg, w_gla, w_swa)
    return pl.pallas_call(
        _outproj_kernel,
        out_shape=jax.ShapeDtypeStruct((T_ALL, D_MODEL), F32),
        grid=(T_ALL // TOK_TILE,),
        in_specs=[_row_spec(D_MODEL), _row_spec(GLA_VAL), _row_spec(GLA_VAL), _row_spec(SWA_W)]
        + [_full_spec(c.shape) for c in consts],
        out_specs=_row_spec(D_MODEL),
        compiler_params=_params("parallel"),
        name="outproj",
    )(x, og, r, osw, *consts)


def _ffn_kernel(x_ref, g_ref, wg_ref, wu_ref, wd_ref, y_ref):
    x = x_ref[...]
    h = _rms(x, g_ref[...]).astype(BF16)
    y_ref[...] = x
    for c in range(D_FF // FF_CHUNK):
        cs = slice(c * FF_CHUNK, (c + 1) * FF_CHUNK)
        act = (_silu(_dot(h, wg_ref[:, cs])) * _dot(h, wu_ref[:, cs])).astype(BF16)
        y_ref[...] += _dot(act, wd_ref[cs, :])


def _ffn(x, g, wg, wu, wd):
    consts = (g, wg, wu, wd)
    return pl.pallas_call(
        _ffn_kernel,
        out_shape=jax.ShapeDtypeStruct((T_ALL, D_MODEL), F32),
        grid=(T_ALL // TOK_TILE,),
        in_specs=[_row_spec(D_MODEL)] + [_full_spec(c.shape) for c in consts],
        out_specs=_row_spec(D_MODEL),
        compiler_params=_params("parallel"),
        name="ffn_dense",
    )(x, *consts)


def _router_kernel(x_ref, g_ref, wr_ref, ids_ref, gates_ref):
    h = _rms(x_ref[...], g_ref[...])
    logits = jnp.dot(h, wr_ref[...], preferred_element_type=F32, precision=lax.Precision.HIGHEST)
    lane = lax.broadcasted_iota(jnp.int32, logits.shape, 1)
    logits = jnp.where(lane < N_EXPERTS, logits, -jnp.inf)
    m1 = jnp.max(logits, axis=-1, keepdims=True)
    i1 = jnp.min(jnp.where(logits == m1, lane, 128), axis=-1, keepdims=True)
    rest = jnp.where(lane == i1, -jnp.inf, logits)
    m2 = jnp.max(rest, axis=-1, keepdims=True)
    i2 = jnp.min(jnp.where(rest == m2, lane, 128), axis=-1, keepdims=True)
    e2 = jnp.exp(m2 - m1)
    two = lax.broadcasted_iota(jnp.int32, ids_ref.shape, 1)
    ids_ref[...] = jnp.where(two == 0, i1, i2)
    gates_ref[...] = jnp.where(two == 0, 1.0, e2) / (1.0 + e2)


def _router(x, g, wr):
    return pl.pallas_call(
        _router_kernel,
        out_shape=[jax.ShapeDtypeStruct((T_ALL, 2), jnp.int32), jax.ShapeDtypeStruct((T_ALL, 2), F32)],
        grid=(T_ALL // TOK_TILE,),
        in_specs=[_row_spec(D_MODEL), _full_spec(g.shape), _full_spec(wr.shape)],
        out_specs=[_row_spec(2), _row_spec(2)],
        compiler_params=_params("parallel"),
        name="router_top2",
    )(x, g, wr)


MOE_ROWS = 2 * T_ALL + N_EXPERTS * MOE_TILE
MOE_TILES = MOE_ROWS // MOE_TILE


def _dispatch_plan(ids, gates):
    e = ids.reshape(-1)
    onehot = (e[:, None] == jnp.arange(N_EXPERTS)[None, :]).astype(jnp.int32)
    rank = jnp.take_along_axis(jnp.cumsum(onehot, axis=0) - onehot, e[:, None], axis=1)[:, 0]
    counts = jnp.sum(onehot, axis=0)
    padded = ((counts + MOE_TILE - 1) // MOE_TILE) * MOE_TILE
    ends = jnp.cumsum(padded)
    pos = (ends - padded)[e] + rank
    pair = jnp.arange(2 * T_ALL, dtype=jnp.int32)
    tok = jnp.zeros((MOE_ROWS,), jnp.int32).at[pos].set(pair // 2)
    dst = jnp.full((MOE_ROWS,), -1, jnp.int32).at[pos].set((pair % 2) * T_ALL + pair // 2)
    gate = jnp.zeros((MOE_ROWS,), F32).at[pos].set(gates.reshape(-1))
    tile_start = jnp.arange(MOE_TILES, dtype=jnp.int32) * MOE_TILE
    tile_expert = jnp.minimum(jnp.searchsorted(ends, tile_start, side="right"), N_EXPERTS - 1).astype(jnp.int32)
    n_used = (ends[-1] // MOE_TILE).astype(jnp.int32).reshape(1)
    return (tile_expert, n_used, tok.reshape(MOE_TILES, 1, MOE_TILE), dst.reshape(MOE_TILES, 1, MOE_TILE),
            gate.reshape(MOE_ROWS, 1))


def _moe_kernel(te_ref, nu_ref, tok_ref, dst_ref, x_hbm, gate_ref, g_ref, wg_ref, wu_ref, wd_ref,
                y_hbm, xbuf, ybuf, sem_in, sem_out):
    del te_ref

    def gather_row(r):
        return pltpu.make_async_copy(x_hbm.at[pl.ds(tok_ref[0, 0, r], 1), :], xbuf.at[pl.ds(r, 1), :], sem_in)

    def scatter_row(r, dst):
        return pltpu.make_async_copy(ybuf.at[pl.ds(r, 1), :], y_hbm.at[pl.ds(dst, 1), :], sem_out)

    @pl.when(pl.program_id(0) < nu_ref[0])
    def _():
        @pl.loop(0, MOE_TILE)
        def _(r):
            gather_row(r).start()

        @pl.loop(0, MOE_TILE)
        def _(r):
            gather_row(r).wait()

        h = _rms(xbuf[...], g_ref[...]).astype(BF16)
        ybuf[...] = jnp.zeros_like(ybuf)
        for c in range(D_EXPERT // EXP_CHUNK):
            cs = slice(c * EXP_CHUNK, (c + 1) * EXP_CHUNK)
            act = (_silu(_dot(h, wg_ref[0, :, cs])) * _dot(h, wu_ref[0, :, cs])).astype(BF16)
            ybuf[...] += _dot(act, wd_ref[0, cs, :])
        ybuf[...] = ybuf[...] * gate_ref[...]

        @pl.loop(0, MOE_TILE)
        def _(r):
            dst = dst_ref[0, 0, r]

            @pl.when(dst >= 0)
            def _():
                scatter_row(r, dst).start()

        @pl.loop(0, MOE_TILE)
        def _(r):
            dst = dst_ref[0, 0, r]

            @pl.when(dst >= 0)
            def _():
                scatter_row(r, dst).wait()


def _moe(x, g, plan, wg, wu, wd):
    tile_expert, n_used, tok, dst, gate = plan
    idx = pl.BlockSpec((1, 1, MOE_TILE), lambda i, te, nu: (i, 0, 0), memory_space=pltpu.SMEM)
    up = pl.BlockSpec((1, D_MODEL, D_EXPERT), lambda i, te, nu: (te[i], 0, 0))
    down = pl.BlockSpec((1, D_EXPERT, D_MODEL), lambda i, te, nu: (te[i], 0, 0))
    return pl.pallas_call(
        _moe_kernel,
        out_shape=jax.ShapeDtypeStruct((2 * T_ALL, D_MODEL), F32),
        grid_spec=pltpu.PrefetchScalarGridSpec(
            num_scalar_prefetch=2,
            grid=(MOE_TILES,),
            in_specs=[idx, idx, pl.BlockSpec(memory_space=pl.ANY),
                      pl.BlockSpec((MOE_TILE, 1), lambda i, te, nu: (i, 0)),
                      pl.BlockSpec(g.shape, lambda i, te, nu: (0, 0)), up, up, down],
            out_specs=pl.BlockSpec(memory_space=pl.ANY),
            scratch_shapes=[pltpu.VMEM((MOE_TILE, D_MODEL), F32), pltpu.VMEM((MOE_TILE, D_MODEL), F32),
                            pltpu.SemaphoreType.DMA(()), pltpu.SemaphoreType.DMA(())],
        ),
        compiler_params=_params("arbitrary"),
        name="moe_experts",
    )(tile_expert, n_used, tok, dst, x, gate, g, wg, wu, wd)


def _combine_kernel(x_ref, a_ref, b_ref, y_ref):
    y_ref[...] = x_ref[...] + (a_ref[0] + b_ref[0])


def _combine(x, y2):
    y2 = y2.reshape(2, T_ALL, D_MODEL)
    slot = lambda k: pl.BlockSpec((1, TOK_TILE, D_MODEL), lambda i: (k, i, 0))
    return pl.pallas_call(
        _combine_kernel,
        out_shape=jax.ShapeDtypeStruct((T_ALL, D_MODEL), F32),
        grid=(T_ALL // TOK_TILE,),
        in_specs=[_row_spec(D_MODEL), slot(0), slot(1)],
        out_specs=_row_spec(D_MODEL),
        compiler_params=_params("parallel"),
        name="moe_combine",
    )(x, y2, y2)


def _head_mean_matrix():
    idx = np.arange(SWA_W) // HEAD_DIM
    return jnp.asarray((idx[:, None] == idx[None, :]).astype(np.float32) / HEAD_DIM, dtype=BF16)


def _pad_rows(a, rows):
    return jnp.pad(a, ((0, 0), (0, rows - a.shape[1]), (0, 0)))


def kernel(x_prompt, x_sample, cache_swa_k, cache_swa_v, state_gla, norm_mix, w_in, w_gate_lr, b_gate, g_gla_out, g_q, g_k, g_swa_out, w_out, norm_ffn, w_ffn_gate, w_ffn_up, w_ffn_down, w_router, w_exp_gate, w_exp_up, w_exp_down):
    bd = _head_mean_matrix()
    x = jnp.concatenate([x_prompt.reshape(T_PROMPT, D_MODEL), x_sample.reshape(T_SAMPLE, D_MODEL),
                         jnp.zeros((T_ALL - T_PROMPT - T_SAMPLE, D_MODEL), F32)], axis=0)
    sample = slice(T_PROMPT, T_PROMPT + T_SAMPLE)
    lr0 = 2 * GLA_KEY + 2 * GLA_VAL
    zero_state = jnp.zeros((BATCH, GLA_HEADS, GLA_DV, GLA_DK), F32)
    row = lambda a: a.reshape(1, -1).astype(F32)
    outs = [[] for _ in range(6)]

    for i in range(DEPTH):
        w = w_in[i].astype(BF16)
        q, k, v, r, la, sq, sk, sv, skb, svb = _inproj(
            x, row(norm_mix[i]), w[:, :lr0], w[:, lr0:lr0 + GATE_RANK], w[:, lr0 + GATE_RANK:],
            w_gate_lr[i].astype(BF16), row(b_gate[i]), bd,
            row(jnp.tile(g_q[i], SWA_HEADS)), row(jnp.tile(g_k[i], SWA_HEADS)))

        og_p, st_p = _gla(q, k, la, v, zero_state, batch=BATCH, seq_len=SEQ, seq_block=1024,
                          chunk=GLA_CHUNK, name="gla_prompt")
        dec = lambda a: _pad_rows(a[sample].reshape(DEC_BATCH, DEC_SEQ, -1), DEC_PAD)
        flat = lambda a: a.reshape(DEC_BATCH * DEC_PAD, -1)
        og_s, st_s = _gla(flat(dec(q)), flat(dec(k)), flat(dec(la)), flat(dec(v)),
                          state_gla[i].astype(F32).transpose(0, 1, 3, 2),
                          batch=DEC_BATCH, seq_len=DEC_PAD, seq_block=DEC_PAD, chunk=DEC_PAD, name="gla_decode")
        og_s = og_s.reshape(DEC_BATCH, DEC_PAD, GLA_VAL)[:, :DEC_SEQ].reshape(T_SAMPLE, GLA_VAL)

        band_o, band_l = [], []
        for d in DILATIONS:
            view = lambda a: a[:T_PROMPT].reshape(BATCH, SEQ // d, d * SWA_W)
            o_d, l_d = _band_attn(view(sq), view(skb), view(svb), d)
            band_o.append(o_d.reshape(T_PROMPT, SWA_W))
            band_l.append(l_d.reshape(T_PROMPT, SWA_W))
        g_swa = row(g_swa_out[i])
        os_p = _merge(band_o, band_l, g_swa, bd)

        kc = cache_swa_k[i].astype(F32).reshape(DEC_BATCH, MAX_WINDOW, SWA_W)
        vc = cache_swa_v[i].astype(F32).reshape(DEC_BATCH, MAX_WINDOW, SWA_W)
        os_s = _dec_attn(dec(sq.astype(F32)), kc, vc, dec(sk), dec(sv), g_swa)
        os_s = os_s[:, :DEC_SEQ].reshape(T_SAMPLE, SWA_W)

        tail = T_ALL - T_PROMPT - T_SAMPLE
        og = jnp.concatenate([og_p, og_s, jnp.zeros((tail, GLA_VAL), F32)], axis=0)
        osw = jnp.concatenate([os_p, os_s, jnp.zeros((tail, SWA_W), F32)], axis=0)
        wo = w_out[i].astype(BF16)
        x = _outproj(x, og, r, osw, row(g_gla_out[i]), wo[:GLA_VAL], wo[GLA_VAL:])

        j = i // 2
        if i % 2 == 0:
            x = _ffn(x, row(norm_ffn[i]), w_ffn_gate[j].astype(BF16), w_ffn_up[j].astype(BF16),
                     w_ffn_down[j].astype(BF16))
        else:
            g_ffn = row(norm_ffn[i])
            wr = jnp.pad(w_router[j].astype(F32), ((0, 0), (0, 128 - N_EXPERTS)))
            ids, gates = _router(x, g_ffn, wr)
            y2 = _moe(x, g_ffn, _dispatch_plan(ids, gates), w_exp_gate[j].astype(BF16),
                      w_exp_up[j].astype(BF16), w_exp_down[j].astype(BF16))
            x = _combine(x, y2)

        keep = slice(SEQ - MAX_WINDOW, SEQ)
        heads = lambda a, b: a.reshape(b, -1, SWA_HEADS, HEAD_DIM)
        outs[0].append(heads(sk[:T_PROMPT].reshape(BATCH, SEQ, SWA_W)[:, keep], BATCH))
        outs[1].append(heads(sv[:T_PROMPT].reshape(BATCH, SEQ, SWA_W)[:, keep], BATCH))
        outs[2].append(st_p.transpose(0, 1, 3, 2))
        new_rows = lambda a: a[sample].reshape(DEC_BATCH, DEC_SEQ, SWA_W)
        outs[3].append(heads(jnp.concatenate([kc[:, DEC_SEQ:], new_rows(sk)], axis=1), DEC_BATCH))
        outs[4].append(heads(jnp.concatenate([vc[:, DEC_SEQ:], new_rows(sv)], axis=1), DEC_BATCH))
        outs[5].append(st_s.transpose(0, 1, 3, 2).astype(state_gla.dtype))

    y_prompt = x[:T_PROMPT].reshape(BATCH, SEQ, D_MODEL)
    y_sample = x[sample].reshape(DEC_BATCH, DEC_SEQ, D_MODEL)
    return (y_prompt, y_sample) + tuple(jnp.stack(o) for o in outs)
```
